```python
import jax
import jax.numpy as jnp
from jax import lax
import numpy as np

D_MODEL = 4096
BATCH = 2
SEQ = 8192
DEPTH = 2

HEAD_DIM = 128
ATT_HEADS = 8
ATT_WIDTH = ATT_HEADS * HEAD_DIM
MOBA_BLOCK = 256
MOBA_TOPK = 3
MOBA_QCHUNK = 64
ROPE_THETA = 10000.0
POS_OFFSET_MAX = 1024
SGU_GROUPS = 8
SGU_GROUP_DIM = 128
SGU_WIDTH = SGU_GROUPS * SGU_GROUP_DIM
SGU_CHUNK = 128
CONV_GROUPS = 8
CONV_WIDTH = CONV_GROUPS * 128
CONV_KERNEL = 31
N_BRANCH = 3
BRANCH_WIDTH = 1024
IN_COLS = 3 * ATT_WIDTH + 2 * SGU_WIDTH + 2 * CONV_WIDTH + N_BRANCH * D_MODEL
N_EXPERTS = 16
N_EXPERT_GROUPS = 4
EXPERTS_PER_GROUP = N_EXPERTS // N_EXPERT_GROUPS
TOP_K = 2
D_FF_EXPERT = 1024
EPS = 1e-6

kernel_name = 'hybrid_moba_sgu_conformer_moe_adaln'


def rms_norm(x, g):
    xf = x.astype(jnp.float32)
    y = xf * lax.rsqrt(jnp.mean(xf * xf, axis=-1, keepdims=True) + EPS)
    return (y * g.astype(jnp.float32)).astype(x.dtype)


def layer_norm(x, g, b):
    xf = x.astype(jnp.float32)
    mu = jnp.mean(xf, axis=-1, keepdims=True)
    var = jnp.mean(jnp.square(xf - mu), axis=-1, keepdims=True)
    y = (xf - mu) * lax.rsqrt(var + EPS)
    return (y * g.astype(jnp.float32) + b.astype(jnp.float32)).astype(x.dtype)


def rope(x, positions):
    half = HEAD_DIM // 2
    inv_freq = ROPE_THETA ** (-2.0 * jnp.arange(half, dtype=jnp.float32) / HEAD_DIM)
    ang = positions.astype(jnp.float32)[:, None, :, None] * inv_freq
    cos, sin = jnp.cos(ang), jnp.sin(ang)
    xf = x.astype(jnp.float32)
    x1, x2 = xf[..., :half], xf[..., half:]
    return jnp.concatenate([x1 * cos - x2 * sin, x2 * cos + x1 * sin], axis=-1).astype(x.dtype)


def moba_attention(q, k, v):
    B, H, S, Dh = q.shape
    nb = -(-S // MOBA_BLOCK)
    pad = nb * MOBA_BLOCK - S
    kb = jnp.pad(k, ((0, 0), (0, 0), (0, pad), (0, 0))).reshape(B, H, nb, MOBA_BLOCK, Dh)
    vb = jnp.pad(v, ((0, 0), (0, 0), (0, pad), (0, 0))).reshape(B, H, nb, MOBA_BLOCK, Dh)
    kmean = jnp.mean(kb.astype(jnp.float32), axis=3)
    ksel = min(MOBA_TOPK, nb)
    scale = HEAD_DIM ** -0.5
    bi = jnp.arange(B)[:, None, None, None]
    hi = jnp.arange(H)[None, :, None, None]
    blk_ids = jnp.arange(nb)
    key_off = jnp.arange(MOBA_BLOCK)

    def one_chunk(ci):
        start = ci * MOBA_QCHUNK
        qc = lax.dynamic_slice_in_dim(q, start, MOBA_QCHUNK, axis=2)
        qpos = start + jnp.arange(MOBA_QCHUNK)
        cblk = start // MOBA_BLOCK
        gate = jnp.einsum('bhqd,bhnd->bhqn', qc.astype(jnp.float32), kmean)
        gate = jnp.where(blk_ids < cblk, gate, -jnp.inf)
        _, idx = lax.top_k(gate, ksel)
        sel_valid = jnp.arange(ksel) < cblk
        k_g = kb[bi, hi, idx]
        v_g = vb[bi, hi, idx]
        s_past = jnp.einsum('bhqd,bhqjld->bhqjl', qc, k_g, preferred_element_type=jnp.float32) * scale
        s_past = jnp.where(sel_valid[:, None], s_past, -jnp.inf)
        s_past = s_past.reshape(B, H, MOBA_QCHUNK, ksel * MOBA_BLOCK)
        k_own = lax.dynamic_index_in_dim(kb, cblk, axis=2, keepdims=False)
        v_own = lax.dynamic_index_in_dim(vb, cblk, axis=2, keepdims=False)
        s_own = jnp.einsum('bhqd,bhld->bhql', qc, k_own, preferred_element_type=jnp.float32) * scale
        own_pos = cblk * MOBA_BLOCK + key_off
        s_own = jnp.where(own_pos[None, :] <= qpos[:, None], s_own, -jnp.inf)
        p = jax.nn.softmax(jnp.concatenate([s_past, s_own], axis=-1), axis=-1)
        p_past = p[..., :ksel * MOBA_BLOCK].reshape(B, H, MOBA_QCHUNK, ksel, MOBA_BLOCK).astype(v.dtype)
        p_own = p[..., ksel * MOBA_BLOCK:].astype(v.dtype)
        return (jnp.einsum('bhqjl,bhqjld->bhqd', p_past, v_g)
                + jnp.einsum('bhql,bhld->bhqd', p_own, v_own))

    out = lax.map(one_chunk, jnp.arange(S // MOBA_QCHUNK))
    return out.transpose(1, 0, 3, 2, 4).reshape(B, S, H * Dh)


def spatial_gating(zu, zv, ln_g, ln_b, ws, bs):
    B, S, _ = zu.shape
    u = jax.nn.gelu(zu)
    vv = layer_norm(jax.nn.gelu(zv), ln_g, ln_b)
    vv = vv.reshape(B, S // SGU_CHUNK, SGU_CHUNK, SGU_GROUPS, SGU_GROUP_DIM)
    causal = jnp.tril(jnp.ones((SGU_CHUNK, SGU_CHUNK), dtype=bool))
    w = jnp.where(causal, ws, 0.0)
    mixed = jnp.einsum('gts,bnsgc->bntgc', w, vv) + bs.T[None, None, :, :, None]
    return u * mixed.reshape(B, S, SGU_WIDTH)


def conv_module(za, zb, w_dw, b_dw, ln_g, ln_b):
    y = za * jax.nn.sigmoid(zb)
    y = lax.conv_general_dilated(
        y, w_dw[:, None, :], window_strides=(1,), padding=[(CONV_KERNEL - 1, 0)],
        dimension_numbers=('NWC', 'WIO', 'NWC'), feature_group_count=CONV_WIDTH) + b_dw
    return jax.nn.silu(layer_norm(y, ln_g, ln_b))


def grouped_moe(h, w_router, router_bias, w_up, w_gate, w_down):
    probs = jax.nn.softmax(jnp.einsum('bsd,de->bse', h, w_router, preferred_element_type=jnp.float32), axis=-1)
    sel = probs + router_bias.astype(jnp.float32)
    grp = jnp.max(sel.reshape(sel.shape[:-1] + (N_EXPERT_GROUPS, EXPERTS_PER_GROUP)), axis=-1)
    g_best = jnp.argmax(grp, axis=-1)
    in_group = (jnp.arange(N_EXPERTS) // EXPERTS_PER_GROUP) == g_best[..., None]
    _, idx = lax.top_k(jnp.where(in_group, sel, -jnp.inf), TOP_K)
    w_sel = jnp.take_along_axis(probs, idx, axis=-1)
    w_sel = w_sel / jnp.sum(w_sel, axis=-1, keepdims=True)
    combine = jnp.sum(jax.nn.one_hot(idx, N_EXPERTS, dtype=jnp.float32) * w_sel[..., None], axis=-2).astype(h.dtype)
    y = jnp.zeros_like(h)
    for e in range(N_EXPERTS):
        he = jax.nn.silu(h @ w_gate[e]) * (h @ w_up[e])
        y = y + combine[..., e:e + 1] * (he @ w_down[e])
    return y


def setup_inputs(seed: int = 0) -> dict:
    key = jax.random.key(seed)
    ks = jax.random.split(key, 24)
    f32 = jnp.float32

    def nrm(k, shape, scale):
        return jax.random.normal(k, shape, f32) * scale

    x = nrm(ks[0], (BATCH, SEQ, D_MODEL), 1.0)
    c = nrm(ks[1], (BATCH, D_MODEL), 1.0)
    offsets = jax.random.randint(ks[2], (BATCH, 1), 0, POS_OFFSET_MAX, dtype=jnp.int32)
    positions = offsets + jnp.arange(SEQ, dtype=jnp.int32)[None, :]
    w_ada = nrm(ks[3], (DEPTH, D_MODEL, 6 * D_MODEL), 0.5 * D_MODEL ** -0.5)
    b_ada = nrm(ks[4], (DEPTH, 6 * D_MODEL), 0.02)
    g_mix = 1.0 + nrm(ks[5], (DEPTH, D_MODEL), 0.05)
    w_in = nrm(ks[6], (DEPTH, D_MODEL, IN_COLS), D_MODEL ** -0.5)
    sgu_ln_g = 1.0 + nrm(ks[7], (DEPTH, SGU_WIDTH), 0.05)
    sgu_ln_b = nrm(ks[8], (DEPTH, SGU_WIDTH), 0.02)
    sgu_w = nrm(ks[9], (DEPTH, SGU_GROUPS, SGU_CHUNK, SGU_CHUNK), SGU_CHUNK ** -0.5)
    sgu_b = 1.0 + nrm(ks[10], (DEPTH, SGU_GROUPS, SGU_CHUNK), 0.1)
    conv_w = nrm(ks[11], (DEPTH, CONV_KERNEL, CONV_WIDTH), CONV_KERNEL ** -0.5)
    conv_b = nrm(ks[12], (DEPTH, CONV_WIDTH), 0.02)
    conv_ln_g = 1.0 + nrm(ks[13], (DEPTH, CONV_WIDTH), 0.05)
    conv_ln_b = nrm(ks[14], (DEPTH, CONV_WIDTH), 0.02)
    w_branch = nrm(ks[15], (DEPTH, N_BRANCH, BRANCH_WIDTH, D_MODEL), BRANCH_WIDTH ** -0.5)
    w_out = nrm(ks[16], (DEPTH, D_MODEL, D_MODEL), D_MODEL ** -0.5)
    g_ffn = 1.0 + nrm(ks[17], (DEPTH, D_MODEL), 0.05)
    w_router = nrm(ks[18], (D_MODEL, N_EXPERTS), D_MODEL ** -0.5)
    router_bias = nrm(ks[19], (N_EXPERTS,), 0.01)
    w_up = nrm(ks[20], (DEPTH, N_EXPERTS, D_MODEL, D_FF_EXPERT), D_MODEL ** -0.5)
    w_gate = nrm(ks[21], (DEPTH, N_EXPERTS, D_MODEL, D_FF_EXPERT), D_MODEL ** -0.5)
    w_down = nrm(ks[22], (DEPTH, N_EXPERTS, D_FF_EXPERT, D_MODEL), D_FF_EXPERT ** -0.5)
    g_final = 1.0 + nrm(ks[23], (D_MODEL,), 0.05)
    return {'x': x, 'c': c, 'positions': positions, 'w_ada': w_ada, 'b_ada': b_ada,
            'g_mix': g_mix, 'w_in': w_in, 'sgu_ln_g': sgu_ln_g, 'sgu_ln_b': sgu_ln_b,
            'sgu_w': sgu_w, 'sgu_b': sgu_b, 'conv_w': conv_w, 'conv_b': conv_b,
            'conv_ln_g': conv_ln_g, 'conv_ln_b': conv_ln_b, 'w_branch': w_branch,
            'w_out': w_out, 'g_ffn': g_ffn, 'w_router': w_router, 'router_bias': router_bias,
            'w_up': w_up, 'w_gate': w_gate, 'w_down': w_down, 'g_final': g_final}


def reference(x, c, positions, w_ada, b_ada, g_mix, w_in, sgu_ln_g, sgu_ln_b, sgu_w, sgu_b,
              conv_w, conv_b, conv_ln_g, conv_ln_b, w_branch, w_out, g_ffn, w_router,
              router_bias, w_up, w_gate, w_down, g_final):
    B, S, D = x.shape
    widths = [ATT_WIDTH] * 3 + [SGU_WIDTH] * 2 + [CONV_WIDTH] * 2
    splits = np.cumsum(widths).tolist()
    cond = jax.nn.silu(c)
    for l in range(DEPTH):
        mod = cond @ w_ada[l] + b_ada[l]
        sh1, sc1, gt1, sh2, sc2, gt2 = [m[:, None, :] for m in jnp.split(mod, 6, axis=-1)]
        h = rms_norm(x, g_mix[l]) * (1.0 + sc1) + sh1
        z = h @ w_in[l]
        zq, zk, zv, zu, zsv, za, zb, zg = jnp.split(z, splits, axis=-1)
        q = rope(zq.reshape(B, S, ATT_HEADS, HEAD_DIM).transpose(0, 2, 1, 3), positions)
        k = rope(zk.reshape(B, S, ATT_HEADS, HEAD_DIM).transpose(0, 2, 1, 3), positions)
        v = zv.reshape(B, S, ATT_HEADS, HEAD_DIM).transpose(0, 2, 1, 3)
        o_att = moba_attention(q, k, v)
        o_sgu = spatial_gating(zu, zsv, sgu_ln_g[l], sgu_ln_b[l], sgu_w[l], sgu_b[l])
        o_conv = conv_module(za, zb, conv_w[l], conv_b[l], conv_ln_g[l], conv_ln_b[l])
        gates = jax.nn.sigmoid(zg).reshape(B, S, N_BRANCH, D)
        merged = (gates[:, :, 0] * (o_att @ w_branch[l, 0])
                  + gates[:, :, 1] * (o_sgu @ w_branch[l, 1])
                  + gates[:, :, 2] * (o_conv @ w_branch[l, 2]))
        x = x + gt1 * (merged @ w_out[l])
        h2 = rms_norm(x, g_ffn[l]) * (1.0 + sc2) + sh2
        x = x + gt2 * grouped_moe(h2, w_router, router_bias, w_up[l], w_gate[l], w_down[l])
    return rms_norm(x, g_final)
```

```python
import functools

import numpy as np
import jax
import jax.numpy as jnp
from jax import lax
from jax.experimental import pallas as pl
from jax.experimental.pallas import tpu as pltpu

HEAD_DIM = 128
MOBA_BLOCK = 256
MOBA_TOPK = 3
ROPE_THETA = 10000.0
SGU_CHUNK = 128
SGU_GROUP_DIM = 128
CONV_KERNEL = 31
N_BRANCH = 3
N_EXPERTS = 16
N_EXPERT_GROUPS = 4
EXPERTS_PER_GROUP = N_EXPERTS // N_EXPERT_GROUPS
EPS = 1e-6

V7X_LANES = 128
V7X_SUBLANES = 8
V7X_VMEM_BUDGET_BYTES = 58 * 1024 * 1024

CONV_HALO = 32
CONV_STRIP = 16

F32 = jnp.float32
BF16 = jnp.bfloat16


def _params(n_axes, vmem_bytes):
    vmem = min(V7X_VMEM_BUDGET_BYTES, max(int(vmem_bytes * 1.25) + (4 << 20), 16 << 20))
    return pltpu.CompilerParams(dimension_semantics=("arbitrary",) * n_axes,
                                vmem_limit_bytes=vmem)


def _tile(n, pref):
    t = min(n, pref)
    while n % t:
        t //= 2
    return t


def _mod_kernel(ct_ref, w_ref, b_ref, o_ref, *, n_batch, chunk):
    d, bn = w_ref.shape

    def body(k, accs):
        r0 = pl.multiple_of(k * chunk, chunk)
        w = w_ref[pl.ds(r0, chunk), :]
        cc = ct_ref[pl.ds(r0, chunk), :]
        cc = cc * jax.nn.sigmoid(cc)
        return tuple(accs[b] + w * cc[:, b:b + 1] for b in range(n_batch))

    accs = lax.fori_loop(0, d // chunk, body,
                         tuple(jnp.zeros((chunk, bn), F32) for _ in range(n_batch)))
    for b in range(n_batch):
        o_ref[b:b + 1, :] = jnp.sum(accs[b], axis=0, keepdims=True) + b_ref[...]


def _modulation(c, w_ada, b_ada):
    depth, d, n6 = w_ada.shape
    nb = c.shape[0]
    bn = _tile(n6, 512)
    chunk = 32
    ct = c.T
    out = pl.pallas_call(
        functools.partial(_mod_kernel, n_batch=nb, chunk=chunk),
        grid=(depth, n6 // bn),
        in_specs=[pl.BlockSpec((d, nb), lambda l, j: (0, 0)),
                  pl.BlockSpec((None, d, bn), lambda l, j: (l, 0, j)),
                  pl.BlockSpec((None, 1, bn), lambda l, j: (l, 0, j))],
        out_specs=pl.BlockSpec((None, nb, bn), lambda l, j: (l, 0, j)),
        out_shape=jax.ShapeDtypeStruct((depth, nb, n6), F32),
        compiler_params=_params(2, 2 * d * bn * 4 + d * V7X_LANES * 4),
        name="adaln_mod",
    )(ct, w_ada, b_ada.reshape(depth, 1, n6))
    return out.reshape(depth, nb, 6, d)


def _rms(x, g):
    ms = jnp.mean(x * x, axis=-1, keepdims=True)
    return x * lax.rsqrt(ms + EPS) * g


def _normmod_kernel(x_ref, mod_ref, g_ref, o_ref, *, sh_row, sc_row):
    y = _rms(x_ref[...], g_ref[...])
    sc = mod_ref[sc_row:sc_row + 1, :]
    sh = mod_ref[sh_row:sh_row + 1, :]
    o_ref[...] = (y * (1.0 + sc) + sh).astype(o_ref.dtype)


def _norm_modulate(x, mod_l, g, seq, sh_row, sc_row):
    t, d = x.shape
    tm = _tile(seq, 512)
    tiles_per_seq = seq // tm
    return pl.pallas_call(
        functools.partial(_normmod_kernel, sh_row=sh_row, sc_row=sc_row),
        grid=(t // tm,),
        in_specs=[pl.BlockSpec((tm, d), lambda i: (i, 0)),
                  pl.BlockSpec((None, 6, d), lambda i: (i // tiles_per_seq, 0, 0)),
                  pl.BlockSpec((1, d), lambda i: (0, 0))],
        out_specs=pl.BlockSpec((tm, d), lambda i: (i, 0)),
        out_shape=jax.ShapeDtypeStruct((t, d), BF16),
        compiler_params=_params(1, 2 * tm * d * 4 + 2 * tm * d * 2 + tm * d * 4),
        name="norm_modulate",
    )(x, mod_l, g.reshape(1, d))


def _final_norm_kernel(x_ref, g_ref, o_ref):
    o_ref[...] = _rms(x_ref[...], g_ref[...])


def _final_norm(x, g):
    t, d = x.shape
    tm = _tile(t, 512)
    return pl.pallas_call(
        _final_norm_kernel,
        grid=(t // tm,),
        in_specs=[pl.BlockSpec((tm, d), lambda i: (i, 0)),
                  pl.BlockSpec((1, d), lambda i: (0, 0))],
        out_specs=pl.BlockSpec((tm, d), lambda i: (i, 0)),
        out_shape=jax.ShapeDtypeStruct((t, d), F32),
        compiler_params=_params(1, 5 * tm * d * 4),
        name="final_norm",
    )(x, g.reshape(1, d))


def _rope_table_kernel(pos_ref, freq_ref, sign_ref, cos_ref, sin_ref):
    ang = pos_ref[...].astype(F32) * freq_ref[...]
    cos_ref[...] = jnp.cos(ang)
    sin_ref[...] = jnp.sin(ang) * sign_ref[...]


def _rope_tables(positions):
    t = positions.size
    half = HEAD_DIM // 2
    inv = ROPE_THETA ** (-2.0 * np.arange(half, dtype=np.float32) / HEAD_DIM)
    freq = jnp.asarray(np.concatenate([inv, inv]).astype(np.float32)).reshape(1, HEAD_DIM)
    sign = jnp.asarray(np.concatenate([-np.ones(half), np.ones(half)]).astype(np.float32)).reshape(1, HEAD_DIM)
    tm = _tile(t, 1024)
    return pl.pallas_call(
        _rope_table_kernel,
        grid=(t // tm,),
        in_specs=[pl.BlockSpec((tm, 1), lambda i: (i, 0)),
                  pl.BlockSpec((1, HEAD_DIM), lambda i: (0, 0)),
                  pl.BlockSpec((1, HEAD_DIM), lambda i: (0, 0))],
        out_specs=[pl.BlockSpec((tm, HEAD_DIM), lambda i: (i, 0))] * 2,
        out_shape=[jax.ShapeDtypeStruct((t, HEAD_DIM), F32)] * 2,
        compiler_params=_params(1, 8 * tm * HEAD_DIM * 4),
        name="rope_tables",
    )(positions.reshape(t, 1), freq, sign)


def _mm_kernel(*refs, n_extra, epilogue):
    a_ref, w_ref = refs[0], refs[1]
    extras = refs[2:2 + n_extra]
    o_ref = refs[2 + n_extra]
    wbf_ref = refs[3 + n_extra]

    @pl.when(pl.program_id(1) == 0)
    def _():
        wbf_ref[...] = w_ref[...].astype(BF16)

    acc = jnp.dot(a_ref[...], wbf_ref[...], preferred_element_type=F32)
    epilogue(acc, o_ref, *extras)


def _matmul(a, w_spec_fn, w, k, n_cols, bm, bn, out_dtype, epilogue, extras, extra_specs, name):
    t = a.shape[0]
    ex_bytes = sum(2 * int(np.prod([s for s in sp.block_shape if s is not None])) * 4 for sp in extra_specs)
    vmem = 2 * bm * k * 2 + 2 * k * bn * 4 + k * bn * 2 + 2 * bm * bn * 4 + bm * bn * 4 + ex_bytes
    return pl.pallas_call(
        functools.partial(_mm_kernel, n_extra=len(extras), epilogue=epilogue),
        grid=(n_cols // bn, t // bm),
        in_specs=[pl.BlockSpec((bm, k), lambda j, i: (i, 0)), w_spec_fn(bn)] + list(extra_specs),
        out_specs=pl.BlockSpec((bm, bn), lambda j, i: (i, j)),
        out_shape=jax.ShapeDtypeStruct((t, n_cols), out_dtype),
        scratch_shapes=[pltpu.VMEM((k, bn), BF16)],
        compiler_params=_params(2, vmem),
        name=name,
    )(a, w, *extras)


def _ep_qkv(acc, o_ref, cos_ref, sin_ref, *, n_rope_blocks):
    j = pl.program_id(0)

    @pl.when(j < n_rope_blocks)
    def _():
        cos = cos_ref[...]
        sin = sin_ref[...]
        for h in range(acc.shape[1] // HEAD_DIM):
            xh = acc[:, h * HEAD_DIM:(h + 1) * HEAD_DIM]
            rot = pltpu.roll(xh, HEAD_DIM // 2, 1)
            o_ref[:, h * HEAD_DIM:(h + 1) * HEAD_DIM] = (xh * cos + rot * sin).astype(o_ref.dtype)

    @pl.when(j >= n_rope_blocks)
    def _():
        o_ref[...] = acc.astype(o_ref.dtype)


def _ep_plain(acc, o_ref):
    o_ref[...] = acc.astype(o_ref.dtype)


def _ep_sigmoid(acc, o_ref):
    o_ref[...] = jax.nn.sigmoid(acc).astype(o_ref.dtype)


def _ep_residual(acc, o_ref, x_ref, mod_ref, *, gate_row):
    o_ref[...] = x_ref[...] + mod_ref[gate_row:gate_row + 1, :] * acc


def _dot_nt(a, b):
    return lax.dot_general(a, b, (((1,), (1,)), ((), ())), preferred_element_type=F32)


def _moba_kernel(q_ref, k_ref, v_ref, o_ref, kmean_ref, m_ref, l_ref, acc_ref, *, nb, scale):
    qi = pl.program_id(2)
    blk_rows = MOBA_BLOCK

    @pl.when(qi == 0)
    def _():
        kf = k_ref[...].astype(F32)
        kmean_ref[...] = jnp.mean(kf.reshape(nb, blk_rows, HEAD_DIM), axis=1)

    q = q_ref[...]
    gate = _dot_nt(q, kmean_ref[...].astype(BF16))
    blk = lax.broadcasted_iota(jnp.int32, gate.shape, 1)
    gate = jnp.where(blk < qi, gate, -jnp.inf)
    sels = []
    for _ in range(MOBA_TOPK):
        gmax = jnp.max(gate, axis=1, keepdims=True)
        idx = jnp.min(jnp.where(gate == gmax, blk, nb), axis=1, keepdims=True)
        idx = jnp.where(gmax > -jnp.inf, idx, -1)
        sels.append(idx)
        gate = jnp.where(blk == idx, -jnp.inf, gate)

    own0 = pl.multiple_of(qi * blk_rows, blk_rows)
    s = _dot_nt(q, k_ref[pl.ds(own0, blk_rows), :]) * scale
    row = lax.broadcasted_iota(jnp.int32, s.shape, 0)
    col = lax.broadcasted_iota(jnp.int32, s.shape, 1)
    s = jnp.where(col <= row, s, -jnp.inf)
    m0 = jnp.max(s, axis=1, keepdims=True)
    p = jnp.exp(s - m0)
    m_ref[...] = m0
    l_ref[...] = jnp.sum(p, axis=1, keepdims=True)
    acc_ref[...] = jnp.dot(p.astype(BF16), v_ref[pl.ds(own0, blk_rows), :],
                           preferred_element_type=F32)

    def body(n, carry):
        r0 = pl.multiple_of(n * blk_rows, blk_rows)
        sn = _dot_nt(q, k_ref[pl.ds(r0, blk_rows), :]) * scale
        hit = (sels[0] == n) | (sels[1] == n) | (sels[2] == n)
        sn = jnp.where(hit, sn, -jnp.inf)
        m_prev = m_ref[...]
        m_new = jnp.maximum(m_prev, jnp.max(sn, axis=1, keepdims=True))
        alpha = jnp.exp(m_prev - m_new)
        pn = jnp.exp(sn - m_new)
        l_ref[...] = alpha * l_ref[...] + jnp.sum(pn, axis=1, keepdims=True)
        acc_ref[...] = alpha * acc_ref[...] + jnp.dot(pn.astype(BF16), v_ref[pl.ds(r0, blk_rows), :],
                                                      preferred_element_type=F32)
        m_ref[...] = m_new
        return carry

    lax.fori_loop(0, qi, body, 0)
    o_ref[...] = (acc_ref[...] / l_ref[...]).astype(o_ref.dtype)


def _moba_attention(qkv, n_batch, seq, n_heads):
    nb = seq // MOBA_BLOCK
    tq = MOBA_BLOCK
    width = n_heads * HEAD_DIM
    return pl.pallas_call(
        functools.partial(_moba_kernel, nb=nb, scale=HEAD_DIM ** -0.5),
        grid=(n_batch, n_heads, seq // tq),
        in_specs=[pl.BlockSpec((None, tq, HEAD_DIM), lambda b, h, i: (b, i, h)),
                  pl.BlockSpec((None, seq, HEAD_DIM), lambda b, h, i: (b, 0, n_heads + h)),
                  pl.BlockSpec((None, seq, HEAD_DIM), lambda b, h, i: (b, 0, 2 * n_heads + h))],
        out_specs=pl.BlockSpec((None, tq, HEAD_DIM), lambda b, h, i: (b, i, h)),
        out_shape=jax.ShapeDtypeStruct((n_batch, seq, width), BF16),
        scratch_shapes=[pltpu.VMEM((nb, HEAD_DIM), F32),
                        pltpu.VMEM((tq, 1), F32),
                        pltpu.VMEM((tq, 1), F32),
                        pltpu.VMEM((tq, HEAD_DIM), F32)],
        compiler_params=_params(3, 4 * seq * HEAD_DIM * 2 + 16 * tq * MOBA_BLOCK * 4),
        name="moba_attention",
    )(qkv, qkv, qkv)


def _layer_norm(x, g, b):
    mu = jnp.mean(x, axis=-1, keepdims=True)
    xc = x - mu
    var = jnp.mean(xc * xc, axis=-1, keepdims=True)
    return xc * lax.rsqrt(var + EPS) * g + b


def _sgu_kernel(zu_ref, zv_ref, g_ref, b_ref, w_ref, bs_ref, o_ref, *, n_groups):
    tm = zu_ref.shape[0]
    vv = _layer_norm(jax.nn.gelu(zv_ref[...]), g_ref[...], b_ref[...]).astype(BF16)
    r = lax.broadcasted_iota(jnp.int32, (SGU_CHUNK, SGU_CHUNK), 0)
    c = lax.broadcasted_iota(jnp.int32, (SGU_CHUNK, SGU_CHUNK), 1)
    for g in range(n_groups):
        wg = jnp.where(r >= c, w_ref[g], 0.0).astype(BF16)
        bg = bs_ref[g]
        cs = slice(g * SGU_GROUP_DIM, (g + 1) * SGU_GROUP_DIM)
        for ch in range(tm // SGU_CHUNK):
            rs = slice(ch * SGU_CHUNK, (ch + 1) * SGU_CHUNK)
            mixed = jnp.dot(wg, vv[rs, cs], preferred_element_type=F32) + bg
            o_ref[rs, cs] = (jax.nn.gelu(zu_ref[rs, cs]) * mixed).astype(o_ref.dtype)


def _spatial_gating(z4, ln_g, ln_b, ws, bs, width):
    t = z4.shape[0]
    n_groups = width // SGU_GROUP_DIM
    tm = _tile(t, 512)
    return pl.pallas_call(
        functools.partial(_sgu_kernel, n_groups=n_groups),
        grid=(t // tm,),
        in_specs=[pl.BlockSpec((tm, width), lambda i: (i, 0)),
                  pl.BlockSpec((tm, width), lambda i: (i, 1)),
                  pl.BlockSpec((1, width), lambda i: (0, 0)),
                  pl.BlockSpec((1, width), lambda i: (0, 0)),
                  pl.BlockSpec((n_groups, SGU_CHUNK, SGU_CHUNK), lambda i: (0, 0, 0)),
                  pl.BlockSpec((n_groups, SGU_CHUNK, 1), lambda i: (0, 0, 0))],
        out_specs=pl.BlockSpec((tm, width), lambda i: (i, 0)),
        out_shape=jax.ShapeDtypeStruct((t, width), BF16),
        compiler_params=_params(1, 10 * tm * width * 4),
        name="spatial_gating",
    )(z4, z4, ln_g.reshape(1, width), ln_b.reshape(1, width), ws, bs.reshape(n_groups, SGU_CHUNK, 1))


def _conv_kernel(za_ref, zb_ref, zah_ref, zbh_ref, w_ref, bdw_ref, g_ref, b_ref, o_ref, ypad_ref,
                 *, tiles_per_seq):
    tm = za_ref.shape[0]
    i = pl.program_id(0)
    halo = zah_ref[...] * jax.nn.sigmoid(zbh_ref[...])
    halo = jnp.where(i % tiles_per_seq == 0, 0.0, halo)
    ypad_ref[0:CONV_HALO, :] = halo
    ypad_ref[CONV_HALO:, :] = za_ref[...] * jax.nn.sigmoid(zb_ref[...])
    lead = CONV_HALO - (CONV_KERNEL - 1)

    def body(s, carry):
        r0 = pl.multiple_of(s * CONV_STRIP, CONV_STRIP)
        win = ypad_ref[pl.ds(r0, CONV_STRIP + CONV_HALO), :]
        acc = jnp.zeros((CONV_STRIP, win.shape[1]), F32)
        for j in range(CONV_KERNEL):
            acc = acc + w_ref[j:j + 1, :] * win[lead + j:lead + j + CONV_STRIP, :]
        y = _layer_norm(acc + bdw_ref[...], g_ref[...], b_ref[...])
        o_ref[pl.ds(r0, CONV_STRIP), :] = (y * jax.nn.sigmoid(y)).astype(o_ref.dtype)
        return carry

    lax.fori_loop(0, tm // CONV_STRIP, body, 0)


def _conv_module(z4, w_dw, b_dw, ln_g, ln_b, width, seq):
    t = z4.shape[0]
    tm = _tile(seq, 256)
    tiles_per_seq = seq // tm
    hb = tm // CONV_HALO

    def halo_map(col):
        return lambda i: (jnp.maximum(i * hb - 1, 0), col)

    return pl.pallas_call(
        functools.partial(_conv_kernel, tiles_per_seq=tiles_per_seq),
        grid=(t // tm,),
        in_specs=[pl.BlockSpec((tm, width), lambda i: (i, 2)),
                  pl.BlockSpec((tm, width), lambda i: (i, 3)),
                  pl.BlockSpec((CONV_HALO, width), halo_map(2)),
                  pl.BlockSpec((CONV_HALO, width), halo_map(3)),
                  pl.BlockSpec((CONV_KERNEL, width), lambda i: (0, 0)),
                  pl.BlockSpec((1, width), lambda i: (0, 0)),
                  pl.BlockSpec((1, width), lambda i: (0, 0)),
                  pl.BlockSpec((1, width), lambda i: (0, 0))],
        out_specs=pl.BlockSpec((tm, width), lambda i: (i, 0)),
        out_shape=jax.ShapeDtypeStruct((t, width), BF16),
        scratch_shapes=[pltpu.VMEM((tm + CONV_HALO, width), F32)],
        compiler_params=_params(1, 8 * tm * width * 4),
        name="conv_module",
    )(z4, z4, z4, z4, w_dw, b_dw.reshape(1, width), ln_g.reshape(1, width), ln_b.reshape(1, width))


def _merge_kernel(oa_ref, os_ref, oc_ref, ga_ref, gs_ref, gc_ref, w_ref, o_ref, wbf_ref):
    @pl.when(pl.program_id(1) == 0)
    def _():
        wbf_ref[...] = w_ref[...].astype(BF16)

    acc = ga_ref[...].astype(F32) * jnp.dot(oa_ref[...], wbf_ref[0], preferred_element_type=F32)
    acc = acc + gs_ref[...].astype(F32) * jnp.dot(os_ref[...], wbf_ref[1], preferred_element_type=F32)
    acc = acc + gc_ref[...].astype(F32) * jnp.dot(oc_ref[...], wbf_ref[2], preferred_element_type=F32)
    o_ref[...] = acc.astype(o_ref.dtype)


def _merge(o_att, o_sgu, o_conv, gates, w_branch_l, d):
    t, bw = o_att.shape
    bm = _tile(t, 1024)
    bn = _tile(d, 512)
    nj = d // bn
    o_spec = pl.BlockSpec((bm, bw), lambda j, i: (i, 0))

    def gate_spec(br):
        return pl.BlockSpec((bm, bn), lambda j, i: (i, br * nj + j))

    vmem = 6 * bm * bw * 2 + 6 * bm * bn * 2 + 2 * N_BRANCH * bw * bn * 4 + N_BRANCH * bw * bn * 2 + 6 * bm * bn * 4
    return pl.pallas_call(
        _merge_kernel,
        grid=(nj, t // bm),
        in_specs=[o_spec, o_spec, o_spec, gate_spec(0), gate_spec(1), gate_spec(2),
                  pl.BlockSpec((N_BRANCH, bw, bn), lambda j, i: (0, 0, j))],
        out_specs=pl.BlockSpec((bm, bn), lambda j, i: (i, j)),
        out_shape=jax.ShapeDtypeStruct((t, d), BF16),
        scratch_shapes=[pltpu.VMEM((N_BRANCH, bw, bn), BF16)],
        compiler_params=_params(2, vmem),
        name="branch_merge",
    )(o_att, o_sgu, o_conv, gates, gates, gates, w_branch_l)


def _router_kernel(x_ref, mod_ref, g_ref, wr_ref, rb_ref, h_ref, ri_ref, w1_ref, w2_ref, cnt_ref, carry_ref,
                   *, sh_row, sc_row):
    i = pl.program_id(0)
    tm = x_ref.shape[0]

    @pl.when(i == 0)
    def _():
        carry_ref[...] = jnp.zeros_like(carry_ref)

    y = _rms(x_ref[...], g_ref[...])
    h = (y * (1.0 + mod_ref[sc_row:sc_row + 1, :]) + mod_ref[sh_row:sh_row + 1, :]).astype(BF16)
    h_ref[...] = h

    logits = jnp.dot(h, wr_ref[...].astype(BF16), preferred_element_type=F32)
    lane = lax.broadcasted_iota(jnp.int32, logits.shape, 1)
    valid = lane < N_EXPERTS
    logits = jnp.where(valid, logits, -jnp.inf)
    e = jnp.exp(logits - jnp.max(logits, axis=1, keepdims=True))
    probs = e / jnp.sum(e, axis=1, keepdims=True)
    sel = jnp.where(valid, probs + rb_ref[...], -jnp.inf)

    grp = lane // EXPERTS_PER_GROUP
    best_val = jnp.full((tm, 1), -jnp.inf, F32)
    best_grp = jnp.zeros((tm, 1), jnp.int32)
    for gi in range(N_EXPERT_GROUPS):
        gmax = jnp.max(jnp.where(grp == gi, sel, -jnp.inf), axis=1, keepdims=True)
        better = gmax > best_val
        best_grp = jnp.where(better, gi, best_grp)
        best_val = jnp.where(better, gmax, best_val)
    cand = jnp.where(grp == best_grp, sel, -jnp.inf)
    m1 = jnp.max(cand, axis=1, keepdims=True)
    i1 = jnp.min(jnp.where(cand == m1, lane, V7X_LANES), axis=1, keepdims=True)
    cand = jnp.where(lane == i1, -jnp.inf, cand)
    m2 = jnp.max(cand, axis=1, keepdims=True)
    i2 = jnp.min(jnp.where(cand == m2, lane, V7X_LANES), axis=1, keepdims=True)
    p1 = jnp.sum(jnp.where(lane == i1, probs, 0.0), axis=1, keepdims=True)
    p2 = jnp.sum(jnp.where(lane == i2, probs, 0.0), axis=1, keepdims=True)
    den = p1 + p2
    w1_ref[...] = jnp.broadcast_to(p1 / den, w1_ref.shape)
    w2_ref[...] = jnp.broadcast_to(p2 / den, w2_ref.shape)

    onehot = ((lane == i1) | (lane == i2)).astype(BF16)
    r = lax.broadcasted_iota(jnp.int32, (tm, tm), 0)
    c = lax.broadcasted_iota(jnp.int32, (tm, tm), 1)
    before = jnp.dot((c < r).astype(BF16), onehot, preferred_element_type=F32) + carry_ref[...]
    r1 = jnp.sum(jnp.where(lane == i1, before, 0.0), axis=1, keepdims=True).astype(jnp.int32)
    r2 = jnp.sum(jnp.where(lane == i2, before, 0.0), axis=1, keepdims=True).astype(jnp.int32)
    ri_ref[...] = jnp.where(lane == 0, i1, jnp.where(lane == 1, i2, jnp.where(lane == 2, r1, r2)))
    carry_ref[...] = carry_ref[...] + jnp.sum(onehot.astype(F32), axis=0, keepdims=True)
    cnt_ref[...] = carry_ref[...]


def _route(x, mod_l, g, w_router, router_bias, seq, sh_row, sc_row):
    t, d = x.shape
    tm = _tile(seq, 512)
    tiles_per_seq = seq // tm
    wr = jnp.pad(w_router, ((0, 0), (0, V7X_LANES - N_EXPERTS)))
    rb = jnp.pad(router_bias, (0, V7X_LANES - N_EXPERTS)).reshape(1, V7X_LANES)
    row = lambda i: (i, 0)
    return pl.pallas_call(
        functools.partial(_router_kernel, sh_row=sh_row, sc_row=sc_row),
        grid=(t // tm,),
        in_specs=[pl.BlockSpec((tm, d), row),
                  pl.BlockSpec((None, 6, d), lambda i: (i // tiles_per_seq, 0, 0)),
                  pl.BlockSpec((1, d), lambda i: (0, 0)),
                  pl.BlockSpec((d, V7X_LANES), lambda i: (0, 0)),
                  pl.BlockSpec((1, V7X_LANES), lambda i: (0, 0))],
        out_specs=[pl.BlockSpec((tm, d), row),
                   pl.BlockSpec((tm, V7X_LANES), row),
                   pl.BlockSpec((tm, V7X_LANES), row),
                   pl.BlockSpec((tm, V7X_LANES), row),
                   pl.BlockSpec((1, V7X_LANES), lambda i: (0, 0))],
        out_shape=[jax.ShapeDtypeStruct((t, d), BF16),
                   jax.ShapeDtypeStruct((t, V7X_LANES), jnp.int32),
                   jax.ShapeDtypeStruct((t, V7X_LANES), F32),
                   jax.ShapeDtypeStruct((t, V7X_LANES), F32),
                   jax.ShapeDtypeStruct((1, V7X_LANES), F32)],
        scratch_shapes=[pltpu.VMEM((1, V7X_LANES), F32)],
        compiler_params=_params(1, 2 * tm * d * 4 + 2 * tm * d * 2 + 2 * tm * d * 4 + 4 * d * V7X_LANES * 4),
        name="norm_route",
    )(x, mod_l, g.reshape(1, d), wr, rb)


def _row_copy(src_hbm, dst_ref, sem, src_row, dst_row):
    return pltpu.make_async_copy(src_hbm.at[src_row], dst_ref.at[dst_row], sem)


def _gather_kernel(idx_ref, src_hbm, o_ref, sem):
    n = o_ref.shape[0]

    def issue(r, carry):
        _row_copy(src_hbm, o_ref, sem, idx_ref[0, 0, r], r).start()
        return carry

    lax.fori_loop(0, n, issue, 0)

    def drain(r, carry):
        _row_copy(src_hbm, o_ref, sem, 0, r).wait()
        return carry

    lax.fori_loop(0, n, drain, 0)


def _gather_rows(src, idx, tg):
    n, d = src.shape
    r = idx.shape[0]
    sub = d // V7X_LANES
    out = pl.pallas_call(
        _gather_kernel,
        grid=(r // tg,),
        in_specs=[pl.BlockSpec((1, 1, tg), lambda i: (i, 0, 0), memory_space=pltpu.SMEM),
                  pl.BlockSpec(memory_space=pl.ANY)],
        out_specs=pl.BlockSpec((tg, sub, V7X_LANES), lambda i: (i, 0, 0)),
        out_shape=jax.ShapeDtypeStruct((r, sub, V7X_LANES), src.dtype),
        scratch_shapes=[pltpu.SemaphoreType.DMA(())],
        compiler_params=_params(1, 2 * tg * d * src.dtype.itemsize),
        name="dispatch_gather",
    )(idx.reshape(r // tg, 1, tg), src.reshape(n, sub, V7X_LANES))
    return out.reshape(r, d)


def _expert_changed(te_ref, i):
    return (i == 0) | (te_ref[i] != te_ref[jnp.maximum(i - 1, 0)])


def _moe_up_kernel(te_ref, nv_ref, x_ref, wg_ref, wu_ref, o_ref, wgb_ref, wub_ref):
    i = pl.program_id(1)

    @pl.when(_expert_changed(te_ref, i))
    def _():
        wgb_ref[...] = wg_ref[...].astype(BF16)
        wub_ref[...] = wu_ref[...].astype(BF16)

    @pl.when(i < nv_ref[0])
    def _():
        x = x_ref[...]
        a = jnp.dot(x, wgb_ref[...], preferred_element_type=F32)
        u = jnp.dot(x, wub_ref[...], preferred_element_type=F32)
        o_ref[...] = (a * jax.nn.sigmoid(a) * u).astype(o_ref.dtype)

    @pl.when(i >= nv_ref[0])
    def _():
        o_ref[...] = jnp.zeros_like(o_ref)


def _moe_down_kernel(te_ref, nv_ref, h_ref, wd_ref, o_ref, wdb_ref):
    i = pl.program_id(1)

    @pl.when(_expert_changed(te_ref, i))
    def _():
        wdb_ref[...] = wd_ref[...].astype(BF16)

    @pl.when(i < nv_ref[0])
    def _():
        o_ref[...] = jnp.dot(h_ref[...], wdb_ref[...], preferred_element_type=F32)

    @pl.when(i >= nv_ref[0])
    def _():
        o_ref[...] = jnp.zeros_like(o_ref)


def _expert_mlp(xs, tile_expert, n_valid, w_gate_l, w_up_l, w_down_l, tm):
    r, d = xs.shape
    f = w_gate_l.shape[2]
    n_tiles = r // tm
    bn_up = _tile(f, 512)
    bn_dn = _tile(d, 1024)

    def row_map(j, i, te, nv):
        return (jnp.minimum(i, nv[0] - 1), 0)

    up = pl.pallas_call(
        _moe_up_kernel,
        grid_spec=pltpu.PrefetchScalarGridSpec(
            num_scalar_prefetch=2,
            grid=(f // bn_up, n_tiles),
            in_specs=[pl.BlockSpec((tm, d), row_map),
                      pl.BlockSpec((None, d, bn_up), lambda j, i, te, nv: (te[i], 0, j)),
                      pl.BlockSpec((None, d, bn_up), lambda j, i, te, nv: (te[i], 0, j))],
            out_specs=pl.BlockSpec((tm, bn_up), lambda j, i, te, nv: (i, j)),
            scratch_shapes=[pltpu.VMEM((d, bn_up), BF16), pltpu.VMEM((d, bn_up), BF16)]),
        out_shape=jax.ShapeDtypeStruct((r, f), BF16),
        compiler_params=_params(2, 2 * tm * d * 2 + 4 * d * bn_up * 4 + 2 * d * bn_up * 2 + 6 * tm * bn_up * 4),
        name="moe_gate_up",
    )(tile_expert, n_valid, xs, w_gate_l, w_up_l)

    return pl.pallas_call(
        _moe_down_kernel,
        grid_spec=pltpu.PrefetchScalarGridSpec(
            num_scalar_prefetch=2,
            grid=(d // bn_dn, n_tiles),
            in_specs=[pl.BlockSpec((tm, f), row_map),
                      pl.BlockSpec((None, f, bn_dn), lambda j, i, te, nv: (te[i], 0, j))],
            out_specs=pl.BlockSpec((tm, bn_dn), lambda j, i, te, nv: (i, j)),
            scratch_shapes=[pltpu.VMEM((f, bn_dn), BF16)]),
        out_shape=jax.ShapeDtypeStruct((r, d), F32),
        compiler_params=_params(2, 2 * tm * f * 2 + 2 * f * bn_dn * 4 + f * bn_dn * 2 + 3 * tm * bn_dn * 4),
        name="moe_down",
    )(tile_expert, n_valid, up, w_down_l)


def _combine_kernel(p1_ref, p2_ref, x_ref, w1_ref, w2_ref, mod_ref, ys_hbm, o_ref, g1_ref, g2_ref, sem,
                    *, gate_row):
    n = x_ref.shape[0]

    def issue(r, carry):
        _row_copy(ys_hbm, g1_ref, sem.at[0], p1_ref[0, 0, r], r).start()
        _row_copy(ys_hbm, g2_ref, sem.at[1], p2_ref[0, 0, r], r).start()
        return carry

    lax.fori_loop(0, n, issue, 0)

    def drain(r, carry):
        _row_copy(ys_hbm, g1_ref, sem.at[0], 0, r).wait()
        _row_copy(ys_hbm, g2_ref, sem.at[1], 0, r).wait()
        return carry

    lax.fori_loop(0, n, drain, 0)
    y = w1_ref[...] * g1_ref[...] + w2_ref[...] * g2_ref[...]
    o_ref[...] = x_ref[...] + mod_ref[gate_row] * y


def _combine(x, ys, pos1, pos2, w1, w2, mod_l, seq, gate_row):
    t, d = x.shape
    r = ys.shape[0]
    sub = d // V7X_LANES
    tc = _tile(seq, 128)
    tiles_per_seq = seq // tc
    nb = mod_l.shape[0]
    idx_spec = pl.BlockSpec((1, 1, tc), lambda i: (i, 0, 0), memory_space=pltpu.SMEM)
    row3 = pl.BlockSpec((tc, sub, V7X_LANES), lambda i: (i, 0, 0))
    wspec = pl.BlockSpec((tc, 1, V7X_LANES), lambda i: (i, 0, 0))
    out = pl.pallas_call(
        functools.partial(_combine_kernel, gate_row=gate_row),
        grid=(t // tc,),
        in_specs=[idx_spec, idx_spec, row3, wspec, wspec,
                  pl.BlockSpec((None, 6, sub, V7X_LANES), lambda i: (i // tiles_per_seq, 0, 0, 0)),
                  pl.BlockSpec(memory_space=pl.ANY)],
        out_specs=row3,
        out_shape=jax.ShapeDtypeStruct((t, sub, V7X_LANES), F32),
        scratch_shapes=[pltpu.VMEM((tc, sub, V7X_LANES), F32),
                        pltpu.VMEM((tc, sub, V7X_LANES), F32),
                        pltpu.SemaphoreType.DMA((2,))],
        compiler_params=_params(1, 8 * tc * d * 4),
        name="moe_combine",
    )(pos1.reshape(t // tc, 1, tc), pos2.reshape(t // tc, 1, tc),
      x.reshape(t, sub, V7X_LANES), w1.reshape(t, 1, V7X_LANES), w2.reshape(t, 1, V7X_LANES),
      mod_l.reshape(nb, 6, sub, V7X_LANES), ys.reshape(r, sub, V7X_LANES))
    return out.reshape(t, d)


def kernel(x, c, positions, w_ada, b_ada, g_mix, w_in, sgu_ln_g, sgu_ln_b, sgu_w, sgu_b, conv_w, conv_b,
           conv_ln_g, conv_ln_b, w_branch, w_out, g_ffn, w_router, router_bias, w_up, w_gate, w_down,
           g_final):
    n_batch, seq, d = x.shape
    depth = w_ada.shape[0]
    t = n_batch * seq
    bw = w_branch.shape[2]
    n_heads = bw // HEAD_DIM
    assert seq % MOBA_BLOCK == 0 and w_in.shape[2] == 7 * bw + N_BRANCH * d

    mod = _modulation(c, w_ada, b_ada)
    cos, sin = _rope_tables(positions)

    bm = _tile(seq, 1024)
    bn = _tile(bw, 512)
    tiles_per_seq = seq // bm
    qkv_blocks, z4_blocks = 3 * bw // bn, 4 * bw // bn
    tm_e = _tile(seq, 512)
    n_tiles = 2 * t // tm_e + N_EXPERTS
    tok = jnp.arange(t, dtype=jnp.int32)

    xf = x.reshape(t, d)
    for l in range(depth):
        mod_l = mod[l]
        h = _norm_modulate(xf, mod_l, g_mix[l], seq, sh_row=0, sc_row=1)

        def w_in_spec(col0):
            return lambda bn_: pl.BlockSpec((None, d, bn_), lambda j, i: (l, 0, col0 + j))

        tab_spec = pl.BlockSpec((bm, HEAD_DIM), lambda j, i: (i, 0))
        qkv = _matmul(h, w_in_spec(0), w_in, d, 3 * bw, bm, bn, BF16,
                      functools.partial(_ep_qkv, n_rope_blocks=2 * bw // bn),
                      [cos, sin], [tab_spec, tab_spec], "in_proj_qkv")
        z4 = _matmul(h, w_in_spec(qkv_blocks), w_in, d, 4 * bw, bm, bn, F32, _ep_plain, [], [], "in_proj_z")
        gates = _matmul(h, w_in_spec(qkv_blocks + z4_blocks), w_in, d, N_BRANCH * d, bm, bn, BF16,
                        _ep_sigmoid, [], [], "in_proj_gates")

        o_att = _moba_attention(qkv.reshape(n_batch, seq, 3 * bw), n_batch, seq, n_heads).reshape(t, bw)
        o_sgu = _spatial_gating(z4, sgu_ln_g[l], sgu_ln_b[l], sgu_w[l], sgu_b[l], bw)
        o_conv = _conv_module(z4, conv_w[l], conv_b[l], conv_ln_g[l], conv_ln_b[l], bw, seq)
        merged = _merge(o_att, o_sgu, o_conv, gates, w_branch[l], d)

        res_specs = [pl.BlockSpec((bm, bn), lambda j, i: (i, j)),
                     pl.BlockSpec((None, 6, bn), lambda j, i: (i // tiles_per_seq, 0, j))]
        xf = _matmul(merged, lambda bn_: pl.BlockSpec((None, d, bn_), lambda j, i: (l, 0, j)), w_out, d, d,
                     bm, bn, F32, functools.partial(_ep_residual, gate_row=2), [xf, mod_l], res_specs,
                     "out_proj_residual")

        h2, ri, w1, w2, cnt = _route(xf, mod_l, g_ffn[l], w_router, router_bias, seq, sh_row=3, sc_row=4)
        counts = cnt[0, :N_EXPERTS].astype(jnp.int32)
        tiles_e = (counts + tm_e - 1) // tm_e
        tiles_cum = jnp.cumsum(tiles_e)
        offs = (tiles_cum - tiles_e) * tm_e
        pos1 = offs[ri[:, 0]] + ri[:, 2]
        pos2 = offs[ri[:, 1]] + ri[:, 3]
        n_valid = tiles_cum[-1:]
        tile_ids = jnp.minimum(jnp.arange(n_tiles, dtype=jnp.int32), n_valid[0] - 1)
        tile_expert = jnp.sum(tile_ids[:, None] >= tiles_cum[None, :], axis=1).astype(jnp.int32)
        row_token = jnp.zeros((n_tiles * tm_e,), jnp.int32).at[jnp.concatenate([pos1, pos2])].set(
            jnp.concatenate([tok, tok]))

        xs = _gather_rows(h2, row_token, _tile(tm_e, 256))
        ys = _expert_mlp(xs, tile_expert, n_valid, w_gate[l], w_up[l], w_down[l], tm_e)
        xf = _combine(xf, ys, pos1, pos2, w1, w2, mod_l, seq, gate_row=5)

    return _final_norm(xf, g_final).reshape(n_batch, seq, d)
```
